```python
import jax, jax.numpy as jnp
from jax import lax
import numpy as np

D_MODEL = 2048
BATCH = 4
SEQ = 4096
DEPTH = 2

GRID_W = 64
CTX_LEN = 256
N_MIXERS = 2
N_SSD_LAYERS = (DEPTH + 1) // 2
N_SMLP_LAYERS = DEPTH // 2
EPS = 1e-6

SSD_EXPAND = 2
SSD_D_INNER = SSD_EXPAND * D_MODEL
SSD_HEAD_DIM = 64
SSD_N_HEADS = SSD_D_INNER // SSD_HEAD_DIM
SSD_N_GROUPS = 8
SSD_HEADS_PER_GROUP = SSD_N_HEADS // SSD_N_GROUPS
SSD_D_STATE = 128
SSD_CONV_CAUSAL_W = 4
SSD_CONV_W = 2 * (SSD_CONV_CAUSAL_W - 1) + 1
SSD_CHUNK = 128
SSD_CONV_DIM = SSD_D_INNER + 2 * SSD_N_GROUPS * SSD_D_STATE
SSD_IN_DIM = SSD_D_INNER + SSD_CONV_DIM + 2 * SSD_N_HEADS

SMLP_EXPAND = 2
SMLP_D_INNER = SMLP_EXPAND * D_MODEL
SMLP_CHUNK = 128
SMLP_N_GROUPS = 16
SMLP_GROUP_W = SMLP_D_INNER // SMLP_N_GROUPS

kernel_name = 'hybrid_ssd_chunkmlp_prefix_dit'


def rms_norm(x, gain=None):
    xf = x.astype(jnp.float32)
    y = xf * lax.rsqrt(jnp.mean(xf * xf, axis=-1, keepdims=True) + EPS)
    if gain is not None:
        y = y * gain.astype(jnp.float32)
    return y.astype(x.dtype)


def layer_norm(x, gain, bias):
    xf = x.astype(jnp.float32)
    mu = jnp.mean(xf, axis=-1, keepdims=True)
    var = jnp.mean(jnp.square(xf - mu), axis=-1, keepdims=True)
    y = (xf - mu) * lax.rsqrt(var + EPS) * gain.astype(jnp.float32) + bias.astype(jnp.float32)
    return y.astype(x.dtype)


def adaln(cvec, w_mod, b_mod):
    m = jax.nn.silu(cvec) @ w_mod + b_mod
    return jnp.split(m, 3, axis=-1)


def modulate(x, shift, scale):
    return rms_norm(x) * (1 + scale) + shift


def depthwise_conv_centred(x, w, b):
    ch = x.shape[-1]
    y = lax.conv_general_dilated(x, w[:, None, :], window_strides=(1,), padding='SAME',
                                 dimension_numbers=('NWC', 'WIO', 'NWC'),
                                 feature_group_count=ch)
    return y + b


def ssd_chunked(x, dt, a_head, Bm, Cm, state0):
    b, L, H, P = x.shape
    G, N = Bm.shape[-2:]
    R = H // G
    Q = SSD_CHUNK
    nc = L // Q
    dtype = x.dtype
    a = (dt * a_head).reshape(b, nc, Q, G, R)
    a_cum = jnp.cumsum(a, axis=2)
    xdt = (x * dt[..., None].astype(dtype)).reshape(b, nc, Q, G, R, P)
    Bc = Bm.reshape(b, nc, Q, G, N)
    Cc = Cm.reshape(b, nc, Q, G, N)
    diff = a_cum[:, :, :, None] - a_cum[:, :, None]
    mask = jnp.tril(jnp.ones((Q, Q), dtype=bool))[:, :, None, None]
    decay = jnp.exp(jnp.where(mask, diff, -jnp.inf)).astype(dtype)
    cb = jnp.einsum('bcqgn,bckgn->bcqkg', Cc, Bc)
    y_diag = jnp.einsum('bcqkg,bcqkgr,bckgrp->bcqgrp', cb, decay, xdt)
    decay_to_end = jnp.exp(a_cum[:, :, -1:] - a_cum).astype(dtype)
    chunk_states = jnp.einsum('bckgn,bckgr,bckgrp->bcgrpn', Bc, decay_to_end, xdt)
    chunk_decay = jnp.exp(a_cum[:, :, -1])

    def step(s, inp):
        dec, st = inp
        s_next = s * dec[..., None, None] + st.astype(jnp.float32)
        return s_next, s

    s_final, s_in = lax.scan(step, state0.reshape(b, G, R, P, N),
                             (jnp.moveaxis(chunk_decay, 1, 0), jnp.moveaxis(chunk_states, 1, 0)))
    s_in = jnp.moveaxis(s_in, 0, 1).astype(dtype)
    y_off = jnp.einsum('bcqgn,bcgrpn,bcqgr->bcqgrp', Cc, s_in, jnp.exp(a_cum).astype(dtype))
    y = (y_diag + y_off).reshape(b, L, H, P)
    return y, s_final.reshape(b, H, P, N)


def ssd_mixer(h_lat, h_ctx, w_in, conv_w, conv_b, dt_bias, a_log, d_skip, norm_g, w_out, ctx_out):
    H, P, G, N = SSD_N_HEADS, SSD_HEAD_DIM, SSD_N_GROUPS, SSD_D_STATE

    def project(h):
        b, L, _ = h.shape
        z, xbc, dt_raw = jnp.split(h @ w_in, [SSD_D_INNER, SSD_D_INNER + SSD_CONV_DIM], axis=-1)
        xbc = jax.nn.silu(depthwise_conv_centred(xbc, conv_w, conv_b))
        xs, Bm, Cm = jnp.split(xbc, [SSD_D_INNER, SSD_D_INNER + G * N], axis=-1)
        dt = jax.nn.softplus(dt_raw.astype(jnp.float32).reshape(b, L, 2, H)
                             + dt_bias.astype(jnp.float32))
        return (z, xs.reshape(b, L, H, P), Bm.reshape(b, L, G, N), Cm.reshape(b, L, G, N), dt)

    def flip(t):
        return jnp.flip(t, axis=1)

    def finish(z, xs, y_f, y_b_flipped):
        b, L = xs.shape[:2]
        y = y_f + flip(y_b_flipped) + d_skip[:, None].astype(xs.dtype) * xs
        y = y.reshape(b, L, SSD_D_INNER) * jax.nn.silu(z)
        return rms_norm(y, norm_g) @ w_out

    a_head = -jnp.exp(a_log.astype(jnp.float32))
    zc, xc, Bc, Cc, dtc = project(h_ctx)
    zl, xl, Bl, Cl, dtl = project(h_lat)
    zero = jnp.zeros((h_ctx.shape[0], H, P, N), jnp.float32)
    yc_f, sc_f = ssd_chunked(xc, dtc[:, :, 0], a_head[0], Bc, Cc, zero)
    yl_f, _ = ssd_chunked(xl, dtl[:, :, 0], a_head[0], Bl, Cl, sc_f)
    yc_b, sc_b = ssd_chunked(flip(xc), flip(dtc[:, :, 1]), a_head[1], flip(Bc), flip(Cc), zero)
    yl_b, _ = ssd_chunked(flip(xl), flip(dtl[:, :, 1]), a_head[1], flip(Bl), flip(Cl), sc_b)
    out_l = finish(zl, xl, yl_f, yl_b)
    out_c = finish(zc, xc, yc_f, yc_b) if ctx_out else None
    return out_l, out_c


def chunk_mlp(h, w_in, ln_g, ln_b, w_s, b_s, w_out):
    b, L, _ = h.shape
    u, v, g = jnp.split(h @ w_in, 3, axis=-1)
    u = jax.nn.gelu(u)
    v = layer_norm(jax.nn.gelu(v), ln_g, ln_b)
    nc = L // SMLP_CHUNK
    v = v.reshape(b, nc, SMLP_CHUNK, SMLP_N_GROUPS, SMLP_GROUP_W)
    v = jnp.einsum('gqk,bckgw->bcqgw', w_s, v) + b_s.T[None, None, :, :, None]
    s = u * v.reshape(b, L, SMLP_D_INNER)
    return (s * jax.nn.silu(g)) @ w_out


def setup_inputs(seed: int = 0) -> dict:
    key = jax.random.key(seed)
    ks = jax.random.split(key, 24)
    f32 = jnp.float32
    D, H, E = D_MODEL, SSD_N_HEADS, SMLP_D_INNER
    dt0 = jnp.exp(jax.random.uniform(ks[8], (N_SSD_LAYERS, 2, H), f32, np.log(1e-3), np.log(1e-1)))
    return {
        'x': jax.random.normal(ks[0], (BATCH, SEQ, D), f32),
        'c': jax.random.normal(ks[1], (BATCH, D), f32),
        'ctx': jax.random.normal(ks[2], (BATCH, CTX_LEN, D), f32),
        'c_ctx': jax.random.normal(ks[3], (D,), f32),
        'mod_w': jax.random.normal(ks[4], (DEPTH, D, 3 * D), f32) * (0.5 * D ** -0.5),
        'mod_b': jax.random.normal(ks[5], (DEPTH, 3 * D), f32) * 0.01,
        'ssd_w_in': jax.random.normal(ks[6], (N_SSD_LAYERS, D, SSD_IN_DIM), f32) * D ** -0.5,
        'ssd_conv_w': jax.random.normal(ks[7], (N_SSD_LAYERS, SSD_CONV_W, SSD_CONV_DIM), f32) * SSD_CONV_W ** -0.5,
        'ssd_conv_b': jax.random.normal(ks[9], (N_SSD_LAYERS, SSD_CONV_DIM), f32) * 0.01,
        'ssd_dt_bias': dt0 + jnp.log(-jnp.expm1(-dt0)),
        'ssd_a_log': jnp.log(jax.random.uniform(ks[10], (N_SSD_LAYERS, 2, H), f32, 1.0, 16.0)),
        'ssd_d': 1.0 + 0.1 * jax.random.normal(ks[11], (N_SSD_LAYERS, H), f32),
        'ssd_norm_g': 1.0 + 0.1 * jax.random.normal(ks[12], (N_SSD_LAYERS, SSD_D_INNER), f32),
        'ssd_w_out': jax.random.normal(ks[13], (N_SSD_LAYERS, SSD_D_INNER, D), f32) * SSD_D_INNER ** -0.5,
        'smlp_w_in': jax.random.normal(ks[14], (N_SMLP_LAYERS, D, 3 * E), f32) * D ** -0.5,
        'smlp_ln_g': 1.0 + 0.1 * jax.random.normal(ks[15], (N_SMLP_LAYERS, E), f32),
        'smlp_ln_b': 0.01 * jax.random.normal(ks[16], (N_SMLP_LAYERS, E), f32),
        'smlp_w_s': jax.random.normal(ks[17], (N_SMLP_LAYERS, SMLP_N_GROUPS, SMLP_CHUNK, SMLP_CHUNK), f32) * SMLP_CHUNK ** -0.5,
        'smlp_b_s': 1.0 + 0.1 * jax.random.normal(ks[18], (N_SMLP_LAYERS, SMLP_N_GROUPS, SMLP_CHUNK), f32),
        'smlp_w_out': jax.random.normal(ks[19], (N_SMLP_LAYERS, E, D), f32) * E ** -0.5,
        'final_norm_g': 1.0 + 0.1 * jax.random.normal(ks[20], (D,), f32),
    }


def reference(x, c, ctx, c_ctx, mod_w, mod_b, ssd_w_in, ssd_conv_w, ssd_conv_b, ssd_dt_bias,
              ssd_a_log, ssd_d, ssd_norm_g, ssd_w_out, smlp_w_in, smlp_ln_g, smlp_ln_b,
              smlp_w_s, smlp_b_s, smlp_w_out, final_norm_g):
    mixer_of = [i % N_MIXERS for i in range(DEPTH)]
    h_ctx = ctx
    for i in range(DEPTH):
        k = i // N_MIXERS
        ctx_later = any(mixer_of[j] == 0 for j in range(i + 1, DEPTH))
        shift, scale, gate = adaln(c, mod_w[i], mod_b[i])
        hl = modulate(x, shift[:, None], scale[:, None])
        if mixer_of[i] == 0 or ctx_later:
            shift_c, scale_c, gate_c = adaln(c_ctx, mod_w[i], mod_b[i])
            hc = modulate(h_ctx, shift_c, scale_c)
        if mixer_of[i] == 0:
            out_l, out_c = ssd_mixer(hl, hc, ssd_w_in[k], ssd_conv_w[k], ssd_conv_b[k],
                                     ssd_dt_bias[k], ssd_a_log[k], ssd_d[k], ssd_norm_g[k],
                                     ssd_w_out[k], ctx_later)
        else:
            sm = (smlp_w_in[k], smlp_ln_g[k], smlp_ln_b[k], smlp_w_s[k], smlp_b_s[k], smlp_w_out[k])
            out_l = chunk_mlp(hl, *sm)
            out_c = chunk_mlp(hc, *sm) if ctx_later else None
        x = x + gate[:, None] * out_l
        if ctx_later:
            h_ctx = h_ctx + gate_c * out_c
    return rms_norm(x, final_norm_g)
```

```python
import functools

import jax
import jax.numpy as jnp
from jax import lax
from jax.experimental import pallas as pl
from jax.experimental.pallas import tpu as pltpu

F32 = jnp.float32
BF16 = jnp.bfloat16

EPS = 1e-6
D_MODEL = 2048
D_INNER = 4096
N_HEADS = 64
HEAD_DIM = 64
N_GROUPS = 8
HEADS_PER_GROUP = N_HEADS // N_GROUPS
D_STATE = 128
CHUNK = 128
CONV_W = 7
CONV_HALF = CONV_W // 2
BC_DIM = 2 * N_GROUPS * D_STATE
MAIN_DIM = 2 * D_INNER + BC_DIM
SMLP_GROUPS = 16
SMLP_GROUP_W = D_INNER // SMLP_GROUPS
LANES = 128
HALO_ROWS = 16
STAGE_PAD = 8
PAIRS_PER_GROUP = HEADS_PER_GROUP // 2
GROUP_LANES = HEADS_PER_GROUP * HEAD_DIM

ROW_CS = 0
ROW_DT = 128
ROW_W = 256
ROW_DSUM = 384
ROWS_TOTAL = 512

VMEM_LIMIT = 56 * 1024 * 1024


def _cparams(sem):
    return pltpu.CompilerParams(dimension_semantics=sem, vmem_limit_bytes=VMEM_LIMIT)


def _mod_kernel(cs_ref, w_ref, b_ref, o_ref):
    cs = cs_ref[...]
    a = (cs * jax.nn.sigmoid(cs)).astype(BF16)
    w = w_ref[0].astype(BF16)
    o_ref[0] = jnp.dot(a, w, preferred_element_type=F32) + b_ref[0]


def _mod_call(cs, mod_w, mod_b):
    depth, d, n = mod_w.shape
    tn = 1024
    return pl.pallas_call(
        _mod_kernel,
        grid=(depth, n // tn),
        in_specs=[
            pl.BlockSpec((8, d), lambda l, j: (0, 0)),
            pl.BlockSpec((1, d, tn), lambda l, j: (l, 0, j)),
            pl.BlockSpec((1, 1, tn), lambda l, j: (l, 0, j)),
        ],
        out_specs=pl.BlockSpec((1, 8, tn), lambda l, j: (l, 0, j)),
        out_shape=jax.ShapeDtypeStruct((depth, 8, n), F32),
        compiler_params=_cparams(("arbitrary", "arbitrary")),
        name="mod",
    )(cs, mod_w, mod_b.reshape(depth, 1, n))


def _nmm_kernel(with_dt, x_ref, ss_ref, w_ref, *rest):
    if with_dt:
        wdt_ref, o_ref, odt_ref, h_ref = rest
    else:
        o_ref, h_ref = rest
    tm = x_ref.shape[1]
    slab = 256

    @pl.when(pl.program_id(2) == 0)
    def _():
        shift = ss_ref[0, 0:1, :]
        scale1 = 1.0 + ss_ref[0, 1:2, :]

        def body(i, carry):
            r = pl.multiple_of(i * slab, slab)
            x = x_ref[0, pl.ds(r, slab), :]
            ms = jnp.mean(x * x, axis=-1, keepdims=True)
            h = x * lax.rsqrt(ms + EPS) * scale1 + shift
            h_ref[pl.ds(r, slab), :] = h.astype(BF16)
            return carry

        lax.fori_loop(0, tm // slab, body, 0)
        if with_dt:
            odt_ref[0] = jnp.dot(h_ref[...], wdt_ref[...], preferred_element_type=F32)

    o_ref[0] = jnp.dot(h_ref[...], w_ref[...], preferred_element_type=F32).astype(o_ref.dtype)


def _nmm_call(x, ss, w, w_dt=None, tm=1024, tn=1024):
    bx, lx, d = x.shape
    n = w.shape[1]
    with_dt = w_dt is not None
    in_specs = [
        pl.BlockSpec((1, tm, d), lambda b, i, j: (b, i, 0)),
        pl.BlockSpec((1, 2, d), lambda b, i, j: (b, 0, 0)),
        pl.BlockSpec((d, tn), lambda b, i, j: (0, j)),
    ]
    out_specs = [pl.BlockSpec((1, tm, tn), lambda b, i, j: (b, i, j))]
    out_shape = [jax.ShapeDtypeStruct((bx, lx, n), BF16)]
    args = [x, ss, w]
    if with_dt:
        ndt = w_dt.shape[1]
        in_specs.append(pl.BlockSpec((d, ndt), lambda b, i, j: (0, 0)))
        out_specs.append(pl.BlockSpec((1, tm, ndt), lambda b, i, j: (b, i, 0)))
        out_shape.append(jax.ShapeDtypeStruct((bx, lx, ndt), F32))
        args.append(w_dt)
    return pl.pallas_call(
        functools.partial(_nmm_kernel, with_dt),
        grid=(bx, lx // tm, n // tn),
        in_specs=in_specs,
        out_specs=out_specs,
        out_shape=out_shape,
        scratch_shapes=[pltpu.VMEM((tm, d), BF16)],
        compiler_params=_cparams(("arbitrary", "arbitrary", "arbitrary")),
        name="nmm_dt" if with_dt else "nmm",
    )(*args)


def _prefix_sum_rows(x):
    row = lax.broadcasted_iota(jnp.int32, x.shape, 0)
    s = 1
    while s < x.shape[0]:
        x = x + jnp.where(row >= s, pltpu.roll(x, s, 0), 0.0)
        s *= 2
    return x


def _suffix_sum_rows(x):
    q = x.shape[0]
    row = lax.broadcasted_iota(jnp.int32, x.shape, 0)
    s = 1
    while s < q:
        x = x + jnp.where(row < q - s, pltpu.roll(x, q - s, 0), 0.0)
        s *= 2
    return x


def _dt_prep_kernel(dt_ref, bias_ref, alog_ref, cols_ref, rows_ref, dec_ref):
    x = dt_ref[0] + bias_ref[...]
    dt = jnp.maximum(x, 0.0) + jnp.log1p(jnp.exp(-jnp.abs(x)))
    a = dt * (-jnp.exp(alog_ref[...]))
    lane = lax.broadcasted_iota(jnp.int32, a.shape, 1)
    is_fwd = lane < N_HEADS
    cs = jnp.where(is_fwd, _prefix_sum_rows(a), _suffix_sum_rows(a))
    tot = jnp.where(is_fwd[0:1], cs[CHUNK - 1:CHUNK, :], cs[0:1, :])
    w = jnp.exp(tot - cs) * dt
    cols_ref[0] = cs
    dt_t = dt.T
    rows_ref[0, 0, ROW_CS:ROW_CS + 128, :] = cs.T
    rows_ref[0, 0, ROW_DT:ROW_DT + 128, :] = dt_t
    rows_ref[0, 0, ROW_W:ROW_W + 128, :] = w.T
    rows_ref[0, 0, ROW_DSUM:ROW_DSUM + 64, :] = dt_t[0:N_HEADS] + dt_t[N_HEADS:2 * N_HEADS]
    rows_ref[0, 0, ROW_DSUM + 64:ROWS_TOTAL, :] = jnp.zeros((ROWS_TOTAL - ROW_DSUM - 64, CHUNK), F32)
    dec_ref[0, 0] = jnp.broadcast_to(jnp.exp(tot), (LANES, LANES)).T


def _dt_prep_call(dt_raw, bias, alog):
    b, l, _ = dt_raw.shape
    nc = l // CHUNK
    return pl.pallas_call(
        _dt_prep_kernel,
        grid=(b, nc),
        in_specs=[
            pl.BlockSpec((1, CHUNK, LANES), lambda i, c: (i, c, 0)),
            pl.BlockSpec((1, LANES), lambda i, c: (0, 0)),
            pl.BlockSpec((1, LANES), lambda i, c: (0, 0)),
        ],
        out_specs=[
            pl.BlockSpec((1, CHUNK, LANES), lambda i, c: (i, c, 0)),
            pl.BlockSpec((1, 1, ROWS_TOTAL, CHUNK), lambda i, c: (i, c, 0, 0)),
            pl.BlockSpec((1, 1, LANES, LANES), lambda i, c: (i, c, 0, 0)),
        ],
        out_shape=[
            jax.ShapeDtypeStruct((b, l, LANES), F32),
            jax.ShapeDtypeStruct((b, nc, ROWS_TOTAL, CHUNK), F32),
            jax.ShapeDtypeStruct((b, nc, LANES, LANES), F32),
        ],
        compiler_params=_cparams(("arbitrary", "arbitrary")),
        name="dt_prep",
    )(dt_raw, bias, alog)


def _block_diag_pair(x_pair, lane_lo):
    zero = jnp.zeros_like(x_pair)
    return jnp.concatenate(
        [jnp.where(lane_lo, x_pair, zero), jnp.where(lane_lo, zero, x_pair)], axis=0)


def _stage_conv_input(st_ref, main_ref, prev_ref, next_ref, has_prev, has_next):
    prev = prev_ref[0].astype(F32)[HALO_ROWS - STAGE_PAD:HALO_ROWS]
    nxt = next_ref[0].astype(F32)[0:STAGE_PAD]
    st_ref[0:STAGE_PAD, :] = jnp.where(has_prev, prev, 0.0)
    st_ref[STAGE_PAD:STAGE_PAD + CHUNK, :] = main_ref[0].astype(F32)
    st_ref[STAGE_PAD + CHUNK:, :] = jnp.where(has_next, nxt, 0.0)


def _conv_silu(st_ref, w_ref, b_ref, out_ref, width):
    lb = LANES

    def body(i, carry):
        off = pl.multiple_of(i * lb, lb)
        acc = jnp.broadcast_to(b_ref[:, pl.ds(off, lb)], (CHUNK, lb))
        for j in range(CONV_W):
            tap = st_ref[pl.ds(STAGE_PAD - CONV_HALF + j, CHUNK), pl.ds(off, lb)]
            acc = acc + tap * w_ref[j:j + 1, pl.ds(off, lb)]
        y = acc * jax.nn.sigmoid(acc)
        out_ref[0, :, pl.ds(off, lb)] = y.astype(out_ref.dtype)
        return carry

    lax.fori_loop(0, width // lb, body, 0)


def _pass1_kernel(nc, xs_m, xs_p, xs_n, bc_m, bc_p, bc_n, wx_ref, bx_ref, wbc_ref, bbc_ref,
                  rows_ref, dec_ref, s0_ref,
                  xs_act, bc_act, yoffb_ref, sfin_ref,
                  stx_ref, stbc_ref, s_ref):
    ci = pl.program_id(1)
    c = nc - 1 - ci

    @pl.when(ci == 0)
    def _():
        s_ref[...] = s0_ref[0]

    has_prev = c > 0
    has_next = c < nc - 1
    _stage_conv_input(stx_ref, xs_m, xs_p, xs_n, has_prev, has_next)
    _stage_conv_input(stbc_ref, bc_m, bc_p, bc_n, has_prev, has_next)
    _conv_silu(stx_ref, wx_ref, bx_ref, xs_act, D_INNER)
    _conv_silu(stbc_ref, wbc_ref, bbc_ref, bc_act, BC_DIM)

    lane_lo = lax.broadcasted_iota(jnp.int32, (1, LANES), 1) < HEAD_DIM

    def group(g, carry):
        goff = pl.multiple_of(g * GROUP_LANES, GROUP_LANES)
        boff = pl.multiple_of(g * D_STATE, D_STATE)
        b_g = bc_act[0, :, pl.ds(boff, D_STATE)]
        c_g = bc_act[0, :, pl.ds(N_GROUPS * D_STATE + boff, D_STATE)]
        b_t = b_g.astype(F32).T
        s_g = s_ref[:, pl.ds(goff, GROUP_LANES)]
        yoffb_ref[0, :, pl.ds(goff, GROUP_LANES)] = jnp.dot(
            c_g, s_g.astype(BF16), preferred_element_type=F32).astype(BF16)
        for s in range(PAIRS_PER_GROUP):
            off = pl.multiple_of(goff + s * LANES, LANES)
            h0 = g * HEADS_PER_GROUP + 2 * s
            x_bd = _block_diag_pair(xs_act[0, :, pl.ds(off, LANES)], lane_lo)
            w0 = rows_ref[0, 0, pl.ds(ROW_W + N_HEADS + h0, 1), :]
            w1 = rows_ref[0, 0, pl.ds(ROW_W + N_HEADS + h0 + 1, 1), :]
            lhs = jnp.concatenate([b_t * w0, b_t * w1], axis=1).astype(BF16)
            local = jnp.dot(lhs, x_bd, preferred_element_type=F32)
            d0 = dec_ref[0, 0, pl.ds(N_HEADS + h0, 1), :]
            d1 = dec_ref[0, 0, pl.ds(N_HEADS + h0 + 1, 1), :]
            drow = jnp.where(lane_lo, d0, d1)
            s_ref[:, pl.ds(off, LANES)] = s_ref[:, pl.ds(off, LANES)] * drow + local
        return carry

    lax.fori_loop(0, N_GROUPS, group, 0)

    @pl.when(ci == nc - 1)
    def _():
        sfin_ref[0] = s_ref[...]


def _pass1_call(main, rows, dec, s0, wx, bx, wbc, bbc):
    b, l, _ = main.shape
    nc = l // CHUNK
    hb = CHUNK // HALO_ROWS
    nhb = l // HALO_ROWS

    def cidx(ci):
        return nc - 1 - ci

    def prev_idx(ci):
        return jnp.maximum(cidx(ci) * hb - 1, 0)

    def next_idx(ci):
        return jnp.minimum((cidx(ci) + 1) * hb, nhb - 1)

    xs_blk = D_INNER // D_INNER
    bc_blk = 2 * D_INNER // BC_DIM
    in_specs = [
        pl.BlockSpec((1, CHUNK, D_INNER), lambda i, ci: (i, cidx(ci), xs_blk)),
        pl.BlockSpec((1, HALO_ROWS, D_INNER), lambda i, ci: (i, prev_idx(ci), xs_blk)),
        pl.BlockSpec((1, HALO_ROWS, D_INNER), lambda i, ci: (i, next_idx(ci), xs_blk)),
        pl.BlockSpec((1, CHUNK, BC_DIM), lambda i, ci: (i, cidx(ci), bc_blk)),
        pl.BlockSpec((1, HALO_ROWS, BC_DIM), lambda i, ci: (i, prev_idx(ci), bc_blk)),
        pl.BlockSpec((1, HALO_ROWS, BC_DIM), lambda i, ci: (i, next_idx(ci), bc_blk)),
        pl.BlockSpec((CONV_W, D_INNER), lambda i, ci: (0, 0)),
        pl.BlockSpec((1, D_INNER), lambda i, ci: (0, 0)),
        pl.BlockSpec((CONV_W, BC_DIM), lambda i, ci: (0, 0)),
        pl.BlockSpec((1, BC_DIM), lambda i, ci: (0, 0)),
        pl.BlockSpec((1, 1, ROWS_TOTAL, CHUNK), lambda i, ci: (i, cidx(ci), 0, 0)),
        pl.BlockSpec((1, 1, LANES, LANES), lambda i, ci: (i, cidx(ci), 0, 0)),
        pl.BlockSpec((1, D_STATE, D_INNER), lambda i, ci: (i, 0, 0)),
    ]
    out_specs = [
        pl.BlockSpec((1, CHUNK, D_INNER), lambda i, ci: (i, cidx(ci), 0)),
        pl.BlockSpec((1, CHUNK, BC_DIM), lambda i, ci: (i, cidx(ci), 0)),
        pl.BlockSpec((1, CHUNK, D_INNER), lambda i, ci: (i, cidx(ci), 0)),
        pl.BlockSpec((1, D_STATE, D_INNER), lambda i, ci: (i, 0, 0)),
    ]
    out_shape = [
        jax.ShapeDtypeStruct((b, l, D_INNER), BF16),
        jax.ShapeDtypeStruct((b, l, BC_DIM), BF16),
        jax.ShapeDtypeStruct((b, l, D_INNER), BF16),
        jax.ShapeDtypeStruct((b, D_STATE, D_INNER), F32),
    ]
    return pl.pallas_call(
        functools.partial(_pass1_kernel, nc),
        grid=(b, nc),
        in_specs=in_specs,
        out_specs=out_specs,
        out_shape=out_shape,
        scratch_shapes=[
            pltpu.VMEM((CHUNK + 2 * STAGE_PAD, D_INNER), F32),
            pltpu.VMEM((CHUNK + 2 * STAGE_PAD, BC_DIM), F32),
            pltpu.VMEM((D_STATE, D_INNER), F32),
        ],
        compiler_params=_cparams(("arbitrary", "arbitrary")),
        name="ssd_pass1",
    )(main, main, main, main, main, main, wx, bx, wbc, bbc, rows, dec, s0)


def _pass2_kernel(nc, xs_act, bc_act, z_ref, yoffb_ref, cols_ref, rows_ref, dec_ref, s0_ref,
                  dexp_ref, ng_ref,
                  yn_ref, sfin_ref,
                  s_ref, ybuf_ref, ssq_ref):
    c = pl.program_id(1)

    @pl.when(c == 0)
    def _():
        s_ref[...] = s0_ref[0]

    ssq_ref[...] = jnp.zeros_like(ssq_ref)
    lane_lo = lax.broadcasted_iota(jnp.int32, (1, LANES), 1) < HEAD_DIM
    row_i = lax.broadcasted_iota(jnp.int32, (CHUNK, CHUNK), 0)
    col_i = lax.broadcasted_iota(jnp.int32, (CHUNK, CHUNK), 1)
    low = col_i < row_i
    diag = col_i == row_i
    cols = cols_ref[0]

    def row(base, h):
        return rows_ref[0, 0, pl.ds(base + h, 1), :]

    def group(g, carry):
        goff = pl.multiple_of(g * GROUP_LANES, GROUP_LANES)
        boff = pl.multiple_of(g * D_STATE, D_STATE)
        b_g = bc_act[0, :, pl.ds(boff, D_STATE)]
        c_g = bc_act[0, :, pl.ds(N_GROUPS * D_STATE + boff, D_STATE)]
        cb = lax.dot_general(c_g, b_g, (((1,), (1,)), ((), ())),
                             preferred_element_type=F32)
        b_t = b_g.astype(F32).T
        yoff_f = jnp.dot(c_g, s_ref[:, pl.ds(goff, GROUP_LANES)].astype(BF16),
                         preferred_element_type=F32)
        shift = (LANES - g * HEADS_PER_GROUP) % LANES
        cols_g = pltpu.roll(cols, shift, 1)

        def head_m(sidx):
            h = g * HEADS_PER_GROUP + sidx
            c_f = cols_g[:, sidx:sidx + 1]
            c_s = cols_g[:, N_HEADS + sidx:N_HEADS + sidx + 1]
            arg = jnp.where(low, c_f - row(ROW_CS, h), c_s - row(ROW_CS + N_HEADS, h))
            scale = jnp.where(low, row(ROW_DT, h),
                              jnp.where(diag, row(ROW_DSUM, h), row(ROW_DT + N_HEADS, h)))
            m = (cb * jnp.exp(arg) * scale).astype(BF16)
            return m, c_f, c_s, h

        for s in range(PAIRS_PER_GROUP):
            off = pl.multiple_of(goff + s * LANES, LANES)
            m0, cf0, cs0, h0 = head_m(2 * s)
            m1, cf1, cs1, h1 = head_m(2 * s + 1)
            x_pair = xs_act[0, :, pl.ds(off, LANES)]
            x_bd = _block_diag_pair(x_pair, lane_lo)
            lhs_m = jnp.concatenate([m0, m1], axis=1)
            lhs_s = jnp.concatenate([b_t * row(ROW_W, h0), b_t * row(ROW_W, h1)],
                                    axis=1).astype(BF16)
            res = jnp.dot(jnp.concatenate([lhs_m, lhs_s], axis=0), x_bd,
                          preferred_element_type=F32)
            e_f = jnp.exp(jnp.where(lane_lo, cf0, cf1))
            e_b = jnp.exp(jnp.where(lane_lo, cs0, cs1))
            y = (res[0:CHUNK]
                 + yoff_f[:, s * LANES:(s + 1) * LANES] * e_f
                 + yoffb_ref[0, :, pl.ds(off, LANES)].astype(F32) * e_b
                 + dexp_ref[:, pl.ds(off, LANES)] * x_pair.astype(F32))
            drow = jnp.where(lane_lo, dec_ref[0, 0, pl.ds(h0, 1), :],
                             dec_ref[0, 0, pl.ds(h1, 1), :])
            s_ref[:, pl.ds(off, LANES)] = s_ref[:, pl.ds(off, LANES)] * drow + res[CHUNK:]
            zz = z_ref[0, :, pl.ds(off, LANES)].astype(F32)
            gated = y * (zz * jax.nn.sigmoid(zz))
            ybuf_ref[:, pl.ds(off, LANES)] = gated
            ssq_ref[...] += gated * gated
        return carry

    lax.fori_loop(0, N_GROUPS, group, 0)

    ms = jnp.sum(ssq_ref[...], axis=-1, keepdims=True) * (1.0 / D_INNER)
    inv = lax.rsqrt(ms + EPS)
    yn_ref[0] = (ybuf_ref[...] * inv * ng_ref[...]).astype(BF16)

    @pl.when(c == nc - 1)
    def _():
        sfin_ref[0] = s_ref[...]


def _pass2_call(xs_act, bc_act, main, yoffb, cols, rows, dec, s0, dexp, ng):
    b, l, _ = main.shape
    nc = l // CHUNK
    in_specs = [
        pl.BlockSpec((1, CHUNK, D_INNER), lambda i, c: (i, c, 0)),
        pl.BlockSpec((1, CHUNK, BC_DIM), lambda i, c: (i, c, 0)),
        pl.BlockSpec((1, CHUNK, D_INNER), lambda i, c: (i, c, 0)),
        pl.BlockSpec((1, CHUNK, D_INNER), lambda i, c: (i, c, 0)),
        pl.BlockSpec((1, CHUNK, LANES), lambda i, c: (i, c, 0)),
        pl.BlockSpec((1, 1, ROWS_TOTAL, CHUNK), lambda i, c: (i, c, 0, 0)),
        pl.BlockSpec((1, 1, LANES, LANES), lambda i, c: (i, c, 0, 0)),
        pl.BlockSpec((1, D_STATE, D_INNER), lambda i, c: (i, 0, 0)),
        pl.BlockSpec((1, D_INNER), lambda i, c: (0, 0)),
        pl.BlockSpec((1, D_INNER), lambda i, c: (0, 0)),
    ]
    out_specs = [
        pl.BlockSpec((1, CHUNK, D_INNER), lambda i, c: (i, c, 0)),
        pl.BlockSpec((1, D_STATE, D_INNER), lambda i, c: (i, 0, 0)),
    ]
    out_shape = [
        jax.ShapeDtypeStruct((b, l, D_INNER), BF16),
        jax.ShapeDtypeStruct((b, D_STATE, D_INNER), F32),
    ]
    return pl.pallas_call(
        functools.partial(_pass2_kernel, nc),
        grid=(b, nc),
        in_specs=in_specs,
        out_specs=out_specs,
        out_shape=out_shape,
        scratch_shapes=[
            pltpu.VMEM((D_STATE, D_INNER), F32),
            pltpu.VMEM((CHUNK, D_INNER), F32),
            pltpu.VMEM((CHUNK, LANES), F32),
        ],
        compiler_params=_cparams(("arbitrary", "arbitrary")),
        name="ssd_pass2",
    )(xs_act, bc_act, main, yoffb, cols, rows, dec, s0, dexp, ng)


def _oproj_kernel(final, y_ref, w_ref, x_ref, gate_ref, *rest):
    if final:
        g_ref, o_ref = rest
    else:
        (o_ref,) = rest
    acc = jnp.dot(y_ref[0], w_ref[...], preferred_element_type=F32)
    r = x_ref[0] + gate_ref[0] * acc
    if final:
        ms = jnp.mean(r * r, axis=-1, keepdims=True)
        r = r * lax.rsqrt(ms + EPS) * g_ref[...]
    o_ref[0] = r


def _oproj_call(y, w, x, gate, final_g=None, tm=256):
    b, l, k = y.shape
    d = w.shape[1]
    final = final_g is not None
    in_specs = [
        pl.BlockSpec((1, tm, k), lambda i, t: (i, t, 0)),
        pl.BlockSpec((k, d), lambda i, t: (0, 0)),
        pl.BlockSpec((1, tm, d), lambda i, t: (i, t, 0)),
        pl.BlockSpec((1, 1, d), lambda i, t: (i, 0, 0)),
    ]
    args = [y, w, x, gate]
    if final:
        in_specs.append(pl.BlockSpec((1, d), lambda i, t: (0, 0)))
        args.append(final_g)
    return pl.pallas_call(
        functools.partial(_oproj_kernel, final),
        grid=(b, l // tm),
        in_specs=in_specs,
        out_specs=pl.BlockSpec((1, tm, d), lambda i, t: (i, t, 0)),
        out_shape=jax.ShapeDtypeStruct((b, l, d), F32),
        compiler_params=_cparams(("arbitrary", "arbitrary")),
        name="oproj_final" if final else "oproj",
    )(*args)


def _gelu_tanh(x):
    return 0.5 * x * (1.0 + jnp.tanh(0.7978845608028654 * (x + 0.044715 * (x * x * x))))


def _mix_kernel(u_ref, v_ref, g_ref, lng_ref, lnb_ref, ws_ref, bs_ref, o_ref):
    v = _gelu_tanh(v_ref[0].astype(F32))
    mu = jnp.mean(v, axis=-1, keepdims=True)
    vc = v - mu
    var = jnp.mean(vc * vc, axis=-1, keepdims=True)
    vn = (vc * lax.rsqrt(var + EPS) * lng_ref[...] + lnb_ref[...]).astype(BF16)
    for g in range(SMLP_GROUPS):
        sl = slice(g * SMLP_GROUP_W, (g + 1) * SMLP_GROUP_W)
        vs = jnp.dot(ws_ref[g], vn[:, sl], preferred_element_type=F32) + bs_ref[:, sl]
        u = _gelu_tanh(u_ref[0, :, sl].astype(F32))
        gg = g_ref[0, :, sl].astype(F32)
        o_ref[0, :, sl] = (u * vs * (gg * jax.nn.sigmoid(gg))).astype(BF16)


def _mix_call(uvg, ln_g, ln_b, w_s, b_exp):
    b, l, _ = uvg.shape
    nc = l // CHUNK
    return pl.pallas_call(
        _mix_kernel,
        grid=(b, nc),
        in_specs=[
            pl.BlockSpec((1, CHUNK, D_INNER), lambda i, c: (i, c, 0)),
            pl.BlockSpec((1, CHUNK, D_INNER), lambda i, c: (i, c, 1)),
            pl.BlockSpec((1, CHUNK, D_INNER), lambda i, c: (i, c, 2)),
            pl.BlockSpec((1, D_INNER), lambda i, c: (0, 0)),
            pl.BlockSpec((1, D_INNER), lambda i, c: (0, 0)),
            pl.BlockSpec((SMLP_GROUPS, CHUNK, CHUNK), lambda i, c: (0, 0, 0)),
            pl.BlockSpec((CHUNK, D_INNER), lambda i, c: (0, 0)),
        ],
        out_specs=pl.BlockSpec((1, CHUNK, D_INNER), lambda i, c: (i, c, 0)),
        out_shape=jax.ShapeDtypeStruct((b, l, D_INNER), BF16),
        compiler_params=_cparams(("arbitrary", "arbitrary")),
        name="smlp_mix",
    )(uvg, uvg, uvg, ln_g, ln_b, w_s, b_exp)


def _ssd_sequence(main, dt_raw, bias, alog, s0_f, s0_b, conv, dexp, ng):
    wx, bx, wbc, bbc = conv
    cols, rows, dec = _dt_prep_call(dt_raw, bias, alog)
    xs_act, bc_act, yoffb, sfin_b = _pass1_call(main, rows, dec, s0_b, wx, bx, wbc, bbc)
    yn, sfin_f = _pass2_call(xs_act, bc_act, main, yoffb, cols, rows, dec, s0_f, dexp, ng)
    return yn, sfin_f, sfin_b


def kernel(x, c, ctx, c_ctx, mod_w, mod_b, ssd_w_in, ssd_conv_w, ssd_conv_b, ssd_dt_bias,
           ssd_a_log, ssd_d, ssd_norm_g, ssd_w_out, smlp_w_in, smlp_ln_g, smlp_ln_b,
           smlp_w_s, smlp_b_s, smlp_w_out, final_norm_g):
    b, l, d = x.shape
    lc = ctx.shape[1]

    cs = jnp.zeros((8, d), F32).at[:b].set(c).at[b].set(c_ctx)
    mods = _mod_call(cs, mod_w, mod_b)

    ss_lat = mods[0, :b, :2 * d].reshape(b, 2, d)
    ss_ctx = mods[0, b:b + 1, :2 * d].reshape(1, 2, d)
    gate0 = mods[0, :b, 2 * d:].reshape(b, 1, d)
    w_in = ssd_w_in[0]
    w_main = w_in[:, :MAIN_DIM].astype(BF16)
    w_dt = w_in[:, MAIN_DIM:].astype(BF16)
    main_l, dt_l = _nmm_call(x, ss_lat, w_main, w_dt)
    main_c, dt_c = _nmm_call(ctx.reshape(1, b * lc, d), ss_ctx, w_main, w_dt)
    main_c = main_c.reshape(b, lc, MAIN_DIM)
    dt_c = dt_c.reshape(b, lc, 2 * N_HEADS)

    bias = ssd_dt_bias[0].reshape(1, 2 * N_HEADS)
    alog = ssd_a_log[0].reshape(1, 2 * N_HEADS)
    conv_w = ssd_conv_w[0]
    conv_b = ssd_conv_b[0].reshape(1, -1)
    conv = (conv_w[:, :D_INNER], conv_b[:, :D_INNER], conv_w[:, D_INNER:], conv_b[:, D_INNER:])
    dexp = jnp.repeat(ssd_d[0], HEAD_DIM).reshape(1, D_INNER)
    ng = ssd_norm_g[0].reshape(1, D_INNER)

    zero_state = jnp.zeros((b, D_STATE, D_INNER), F32)
    _, sc_f, sc_b = _ssd_sequence(main_c, dt_c, bias, alog, zero_state, zero_state, conv, dexp, ng)
    yn, _, _ = _ssd_sequence(main_l, dt_l, bias, alog, sc_f, sc_b, conv, dexp, ng)
    x1 = _oproj_call(yn, ssd_w_out[0].astype(BF16), x, gate0)

    ss1 = mods[1, :b, :2 * d].reshape(b, 2, d)
    gate1 = mods[1, :b, 2 * d:].reshape(b, 1, d)
    uvg = _nmm_call(x1, ss1, smlp_w_in[0].astype(BF16))[0]
    b_exp = jnp.repeat(smlp_b_s[0].T, SMLP_GROUP_W, axis=1)
    s_mix = _mix_call(uvg, smlp_ln_g[0].reshape(1, -1), smlp_ln_b[0].reshape(1, -1),
                      smlp_w_s[0].astype(BF16), b_exp)
    return _oproj_call(s_mix, smlp_w_out[0].astype(BF16), x1, gate1,
                       final_g=final_norm_g.reshape(1, d))
```

```python
import functools

import jax
import jax.numpy as jnp
from jax import lax
from jax.experimental import pallas as pl
from jax.experimental.pallas import tpu as pltpu

F32 = jnp.float32
BF16 = jnp.bfloat16

EPS = 1e-6
D_MODEL = 2048
D_INNER = 4096
N_HEADS = 64
HEAD_DIM = 64
N_GROUPS = 8
HEADS_PER_GROUP = N_HEADS // N_GROUPS
D_STATE = 128
CHUNK = 128
CONV_W = 7
CONV_HALF = CONV_W // 2
BC_DIM = 2 * N_GROUPS * D_STATE
MAIN_DIM = 2 * D_INNER + BC_DIM
SMLP_GROUPS = 16
SMLP_GROUP_W = D_INNER // SMLP_GROUPS
LANES = 128
HALO_ROWS = 16
STAGE_PAD = 8
PAIRS_PER_GROUP = HEADS_PER_GROUP // 2
GROUP_LANES = HEADS_PER_GROUP * HEAD_DIM

ROW_CS = 0
ROW_DT = 128
ROW_W = 256
ROW_DSUM = 384
ROWS_TOTAL = 512

VMEM_LIMIT = 56 * 1024 * 1024


def _cparams(sem):
    return pltpu.CompilerParams(dimension_semantics=sem, vmem_limit_bytes=VMEM_LIMIT)


def _mod_kernel(cs_ref, w_ref, b_ref, o_ref):
    cs = cs_ref[...]
    a = (cs * jax.nn.sigmoid(cs)).astype(BF16)
    w = w_ref[0].astype(BF16)
    o_ref[0] = jnp.dot(a, w, preferred_element_type=F32) + b_ref[0]


def _mod_call(cs, mod_w, mod_b):
    depth, d, n = mod_w.shape
    tn = 1024
    return pl.pallas_call(
        _mod_kernel,
        grid=(depth, n // tn),
        in_specs=[
            pl.BlockSpec((8, d), lambda l, j: (0, 0)),
            pl.BlockSpec((1, d, tn), lambda l, j: (l, 0, j)),
            pl.BlockSpec((1, 1, tn), lambda l, j: (l, 0, j)),
        ],
        out_specs=pl.BlockSpec((1, 8, tn), lambda l, j: (l, 0, j)),
        out_shape=jax.ShapeDtypeStruct((depth, 8, n), F32),
        compiler_params=_cparams(("arbitrary", "arbitrary")),
        name="mod",
    )(cs, mod_w, mod_b.reshape(depth, 1, n))


def _silu(x):
    return x * jax.nn.sigmoid(x)


def _gelu_tanh(x):
    hx = 0.5 * x
    t = x * (0.7978845608028654 + 0.035677408136300125 * (x * x))
    return hx + hx * jnp.tanh(t)


def _identity(x):
    return x


def _nmm_kernel(with_dt, acts, x_ref, ss_ref, w_ref, *rest):
    if with_dt:
        wdt_ref, o_ref, odt_ref, h_ref = rest
    else:
        o_ref, h_ref = rest
    tm = x_ref.shape[1]
    slab = 256
    j = pl.program_id(2)

    @pl.when(j == 0)
    def _():
        shift = ss_ref[0, 0:1, :]
        scale1 = 1.0 + ss_ref[0, 1:2, :]

        def body(i, carry):
            r = pl.multiple_of(i * slab, slab)
            x = x_ref[0, pl.ds(r, slab), :]
            ms = jnp.mean(x * x, axis=-1, keepdims=True)
            h = x * lax.rsqrt(ms + EPS) * scale1 + shift
            h_ref[pl.ds(r, slab), :] = h.astype(BF16)
            return carry

        lax.fori_loop(0, tm // slab, body, 0)
        if with_dt:
            odt_ref[0] = jnp.dot(h_ref[...], wdt_ref[...], preferred_element_type=F32)

    nblk = pl.num_programs(2)
    for idx, (first, fn) in enumerate(acts):
        last = acts[idx + 1][0] if idx + 1 < len(acts) else nblk

        @pl.when(jnp.logical_and(j >= first, j < last))
        def _(fn=fn):
            r = jnp.dot(h_ref[...], w_ref[...], preferred_element_type=F32)
            o_ref[0] = fn(r).astype(o_ref.dtype)


def _nmm_call(x, ss, w, acts, w_dt=None, tm=1024, tn=1024):
    bx, lx, d = x.shape
    n = w.shape[1]
    with_dt = w_dt is not None
    in_specs = [
        pl.BlockSpec((1, tm, d), lambda b, i, j: (b, i, 0)),
        pl.BlockSpec((1, 2, d), lambda b, i, j: (b, 0, 0)),
        pl.BlockSpec((d, tn), lambda b, i, j: (0, j)),
    ]
    out_specs = [pl.BlockSpec((1, tm, tn), lambda b, i, j: (b, i, j))]
    out_shape = [jax.ShapeDtypeStruct((bx, lx, n), BF16)]
    args = [x, ss, w]
    if with_dt:
        ndt = w_dt.shape[1]
        in_specs.append(pl.BlockSpec((d, ndt), lambda b, i, j: (0, 0)))
        out_specs.append(pl.BlockSpec((1, tm, ndt), lambda b, i, j: (b, i, 0)))
        out_shape.append(jax.ShapeDtypeStruct((bx, lx, ndt), F32))
        args.append(w_dt)
    return pl.pallas_call(
        functools.partial(_nmm_kernel, with_dt, acts),
        grid=(bx, lx // tm, n // tn),
        in_specs=in_specs,
        out_specs=out_specs,
        out_shape=out_shape,
        scratch_shapes=[pltpu.VMEM((tm, d), BF16)],
        compiler_params=_cparams(("arbitrary", "arbitrary", "arbitrary")),
        name="nmm_dt" if with_dt else "nmm",
    )(*args)


def _prefix_sum_rows(x):
    row = lax.broadcasted_iota(jnp.int32, x.shape, 0)
    s = 1
    while s < x.shape[0]:
        x = x + jnp.where(row >= s, pltpu.roll(x, s, 0), 0.0)
        s *= 2
    return x


def _suffix_sum_rows(x):
    q = x.shape[0]
    row = lax.broadcasted_iota(jnp.int32, x.shape, 0)
    s = 1
    while s < q:
        x = x + jnp.where(row < q - s, pltpu.roll(x, q - s, 0), 0.0)
        s *= 2
    return x


def _dt_prep_kernel(dt_ref, bias_ref, alog_ref, cols_ref, rows_ref, dec_ref):
    x = dt_ref[0] + bias_ref[...]
    dt = jnp.maximum(x, 0.0) + jnp.log1p(jnp.exp(-jnp.abs(x)))
    a = dt * (-jnp.exp(alog_ref[...]))
    lane = lax.broadcasted_iota(jnp.int32, a.shape, 1)
    is_fwd = lane < N_HEADS
    cs = jnp.where(is_fwd, _prefix_sum_rows(a), _suffix_sum_rows(a))
    tot = jnp.where(is_fwd[0:1], cs[CHUNK - 1:CHUNK, :], cs[0:1, :])
    w = jnp.exp(tot - cs) * dt
    cols_ref[0] = cs
    dt_t = dt.T
    rows_ref[0, 0, ROW_CS:ROW_CS + 128, :] = cs.T
    rows_ref[0, 0, ROW_DT:ROW_DT + 128, :] = dt_t
    rows_ref[0, 0, ROW_W:ROW_W + 128, :] = w.T
    rows_ref[0, 0, ROW_DSUM:ROW_DSUM + 64, :] = dt_t[0:N_HEADS] + dt_t[N_HEADS:2 * N_HEADS]
    rows_ref[0, 0, ROW_DSUM + 64:ROWS_TOTAL, :] = jnp.zeros((ROWS_TOTAL - ROW_DSUM - 64, CHUNK), F32)
    dec_ref[0, 0] = jnp.broadcast_to(jnp.exp(tot), (LANES, LANES)).T


def _dt_prep_call(dt_raw, bias, alog):
    b, l, _ = dt_raw.shape
    nc = l // CHUNK
    return pl.pallas_call(
        _dt_prep_kernel,
        grid=(b, nc),
        in_specs=[
            pl.BlockSpec((1, CHUNK, LANES), lambda i, c: (i, c, 0)),
            pl.BlockSpec((1, LANES), lambda i, c: (0, 0)),
            pl.BlockSpec((1, LANES), lambda i, c: (0, 0)),
        ],
        out_specs=[
            pl.BlockSpec((1, CHUNK, LANES), lambda i, c: (i, c, 0)),
            pl.BlockSpec((1, 1, ROWS_TOTAL, CHUNK), lambda i, c: (i, c, 0, 0)),
            pl.BlockSpec((1, 1, LANES, LANES), lambda i, c: (i, c, 0, 0)),
        ],
        out_shape=[
            jax.ShapeDtypeStruct((b, l, LANES), F32),
            jax.ShapeDtypeStruct((b, nc, ROWS_TOTAL, CHUNK), F32),
            jax.ShapeDtypeStruct((b, nc, LANES, LANES), F32),
        ],
        compiler_params=_cparams(("arbitrary", "arbitrary")),
        name="dt_prep",
    )(dt_raw, bias, alog)


def _block_diag_pair(x_pair, lane_lo):
    zero = jnp.zeros_like(x_pair)
    return jnp.concatenate(
        [jnp.where(lane_lo, x_pair, zero), jnp.where(lane_lo, zero, x_pair)], axis=0)


def _stage_conv_input(st_ref, main_ref, prev_ref, next_ref, has_prev, has_next):
    prev = prev_ref[0].astype(F32)[HALO_ROWS - STAGE_PAD:HALO_ROWS]
    nxt = next_ref[0].astype(F32)[0:STAGE_PAD]
    st_ref[0:STAGE_PAD, :] = jnp.where(has_prev, prev, 0.0)
    st_ref[STAGE_PAD:STAGE_PAD + CHUNK, :] = main_ref[0].astype(F32)
    st_ref[STAGE_PAD + CHUNK:, :] = jnp.where(has_next, nxt, 0.0)


def _conv_silu(st_ref, w_ref, b_ref, out_ref, width):
    lb = LANES

    def body(i, carry):
        off = pl.multiple_of(i * lb, lb)
        acc = jnp.broadcast_to(b_ref[:, pl.ds(off, lb)], (CHUNK, lb))
        for j in range(CONV_W):
            tap = st_ref[pl.ds(STAGE_PAD - CONV_HALF + j, CHUNK), pl.ds(off, lb)]
            acc = acc + tap * w_ref[j:j + 1, pl.ds(off, lb)]
        y = acc * jax.nn.sigmoid(acc)
        out_ref[0, :, pl.ds(off, lb)] = y.astype(out_ref.dtype)
        return carry

    lax.fori_loop(0, width // lb, body, 0)


def _pass1_kernel(nc, xs_m, xs_p, xs_n, bc_m, bc_p, bc_n, wx_ref, bx_ref, wbc_ref, bbc_ref,
                  rows_ref, dec_ref, s0_ref,
                  xs_act, bc_act, yoffb_ref, sfin_ref,
                  stx_ref, stbc_ref, s_ref):
    ci = pl.program_id(1)
    c = nc - 1 - ci

    @pl.when(ci == 0)
    def _():
        s_ref[...] = s0_ref[0]

    has_prev = c > 0
    has_next = c < nc - 1
    _stage_conv_input(stx_ref, xs_m, xs_p, xs_n, has_prev, has_next)
    _stage_conv_input(stbc_ref, bc_m, bc_p, bc_n, has_prev, has_next)
    _conv_silu(stx_ref, wx_ref, bx_ref, xs_act, D_INNER)
    _conv_silu(stbc_ref, wbc_ref, bbc_ref, bc_act, BC_DIM)

    lane_lo = lax.broadcasted_iota(jnp.int32, (1, LANES), 1) < HEAD_DIM

    def group(g, carry):
        goff = pl.multiple_of(g * GROUP_LANES, GROUP_LANES)
        boff = pl.multiple_of(g * D_STATE, D_STATE)
        b_g = bc_act[0, :, pl.ds(boff, D_STATE)]
        c_g = bc_act[0, :, pl.ds(N_GROUPS * D_STATE + boff, D_STATE)]
        b_t = b_g.astype(F32).T
        s_g = s_ref[:, pl.ds(goff, GROUP_LANES)]
        yoffb_ref[0, :, pl.ds(goff, GROUP_LANES)] = jnp.dot(
            c_g, s_g.astype(BF16), preferred_element_type=F32).astype(BF16)
        for s in range(PAIRS_PER_GROUP):
            off = pl.multiple_of(goff + s * LANES, LANES)
            h0 = g * HEADS_PER_GROUP + 2 * s
            x_bd = _block_diag_pair(xs_act[0, :, pl.ds(off, LANES)], lane_lo)
            w0 = rows_ref[0, 0, pl.ds(ROW_W + N_HEADS + h0, 1), :]
            w1 = rows_ref[0, 0, pl.ds(ROW_W + N_HEADS + h0 + 1, 1), :]
            lhs = jnp.concatenate([b_t * w0, b_t * w1], axis=1).astype(BF16)
            local = jnp.dot(lhs, x_bd, preferred_element_type=F32)
            d0 = dec_ref[0, 0, pl.ds(N_HEADS + h0, 1), :]
            d1 = dec_ref[0, 0, pl.ds(N_HEADS + h0 + 1, 1), :]
            drow = jnp.where(lane_lo, d0, d1)
            s_ref[:, pl.ds(off, LANES)] = s_ref[:, pl.ds(off, LANES)] * drow + local
        return carry

    lax.fori_loop(0, N_GROUPS, group, 0)

    @pl.when(ci == nc - 1)
    def _():
        sfin_ref[0] = s_ref[...]


def _pass1_call(main, rows, dec, s0, wx, bx, wbc, bbc):
    b, l, _ = main.shape
    nc = l // CHUNK
    hb = CHUNK // HALO_ROWS
    nhb = l // HALO_ROWS

    def cidx(ci):
        return nc - 1 - ci

    def prev_idx(ci):
        return jnp.maximum(cidx(ci) * hb - 1, 0)

    def next_idx(ci):
        return jnp.minimum((cidx(ci) + 1) * hb, nhb - 1)

    xs_blk = D_INNER // D_INNER
    bc_blk = 2 * D_INNER // BC_DIM
    in_specs = [
        pl.BlockSpec((1, CHUNK, D_INNER), lambda i, ci: (i, cidx(ci), xs_blk)),
        pl.BlockSpec((1, HALO_ROWS, D_INNER), lambda i, ci: (i, prev_idx(ci), xs_blk)),
        pl.BlockSpec((1, HALO_ROWS, D_INNER), lambda i, ci: (i, next_idx(ci), xs_blk)),
        pl.BlockSpec((1, CHUNK, BC_DIM), lambda i, ci: (i, cidx(ci), bc_blk)),
        pl.BlockSpec((1, HALO_ROWS, BC_DIM), lambda i, ci: (i, prev_idx(ci), bc_blk)),
        pl.BlockSpec((1, HALO_ROWS, BC_DIM), lambda i, ci: (i, next_idx(ci), bc_blk)),
        pl.BlockSpec((CONV_W, D_INNER), lambda i, ci: (0, 0)),
        pl.BlockSpec((1, D_INNER), lambda i, ci: (0, 0)),
        pl.BlockSpec((CONV_W, BC_DIM), lambda i, ci: (0, 0)),
        pl.BlockSpec((1, BC_DIM), lambda i, ci: (0, 0)),
        pl.BlockSpec((1, 1, ROWS_TOTAL, CHUNK), lambda i, ci: (i, cidx(ci), 0, 0)),
        pl.BlockSpec((1, 1, LANES, LANES), lambda i, ci: (i, cidx(ci), 0, 0)),
        pl.BlockSpec((1, D_STATE, D_INNER), lambda i, ci: (i, 0, 0)),
    ]
    out_specs = [
        pl.BlockSpec((1, CHUNK, D_INNER), lambda i, ci: (i, cidx(ci), 0)),
        pl.BlockSpec((1, CHUNK, BC_DIM), lambda i, ci: (i, cidx(ci), 0)),
        pl.BlockSpec((1, CHUNK, D_INNER), lambda i, ci: (i, cidx(ci), 0)),
        pl.BlockSpec((1, D_STATE, D_INNER), lambda i, ci: (i, 0, 0)),
    ]
    out_shape = [
        jax.ShapeDtypeStruct((b, l, D_INNER), BF16),
        jax.ShapeDtypeStruct((b, l, BC_DIM), BF16),
        jax.ShapeDtypeStruct((b, l, D_INNER), BF16),
        jax.ShapeDtypeStruct((b, D_STATE, D_INNER), F32),
    ]
    return pl.pallas_call(
        functools.partial(_pass1_kernel, nc),
        grid=(b, nc),
        in_specs=in_specs,
        out_specs=out_specs,
        out_shape=out_shape,
        scratch_shapes=[
            pltpu.VMEM((CHUNK + 2 * STAGE_PAD, D_INNER), F32),
            pltpu.VMEM((CHUNK + 2 * STAGE_PAD, BC_DIM), F32),
            pltpu.VMEM((D_STATE, D_INNER), F32),
        ],
        compiler_params=_cparams(("arbitrary", "arbitrary")),
        name="ssd_pass1",
    )(main, main, main, main, main, main, wx, bx, wbc, bbc, rows, dec, s0)


def _pass2_kernel(nc, xs_act, bc_act, z_ref, yoffb_ref, cols_ref, rows_ref, dec_ref, s0_ref,
                  dexp_ref, ng_ref,
                  yn_ref, sfin_ref,
                  s_ref, ybuf_ref, ssq_ref):
    c = pl.program_id(1)

    @pl.when(c == 0)
    def _():
        s_ref[...] = s0_ref[0]

    ssq_ref[...] = jnp.zeros_like(ssq_ref)
    lane_lo = lax.broadcasted_iota(jnp.int32, (1, LANES), 1) < HEAD_DIM
    row_i = lax.broadcasted_iota(jnp.int32, (CHUNK, CHUNK), 0)
    col_i = lax.broadcasted_iota(jnp.int32, (CHUNK, CHUNK), 1)
    low = col_i < row_i
    diag = col_i == row_i
    cols = cols_ref[0]

    def row(base, h):
        return rows_ref[0, 0, pl.ds(base + h, 1), :]

    def group(g, carry):
        goff = pl.multiple_of(g * GROUP_LANES, GROUP_LANES)
        boff = pl.multiple_of(g * D_STATE, D_STATE)
        b_g = bc_act[0, :, pl.ds(boff, D_STATE)]
        c_g = bc_act[0, :, pl.ds(N_GROUPS * D_STATE + boff, D_STATE)]
        cb = lax.dot_general(c_g, b_g, (((1,), (1,)), ((), ())),
                             preferred_element_type=F32)
        b_t = b_g.astype(F32).T
        yoff_f = jnp.dot(c_g, s_ref[:, pl.ds(goff, GROUP_LANES)].astype(BF16),
                         preferred_element_type=F32)
        shift = (LANES - g * HEADS_PER_GROUP) % LANES
        cols_g = pltpu.roll(cols, shift, 1)

        def head_m(sidx):
            h = g * HEADS_PER_GROUP + sidx
            c_f = cols_g[:, sidx:sidx + 1]
            c_s = cols_g[:, N_HEADS + sidx:N_HEADS + sidx + 1]
            arg = jnp.where(low, c_f - row(ROW_CS, h), c_s - row(ROW_CS + N_HEADS, h))
            scale = jnp.where(low, row(ROW_DT, h),
                              jnp.where(diag, row(ROW_DSUM, h), row(ROW_DT + N_HEADS, h)))
            m = (cb * jnp.exp(arg) * scale).astype(BF16)
            return m, c_f, c_s, h

        for s in range(PAIRS_PER_GROUP):
            off = pl.multiple_of(goff + s * LANES, LANES)
            m0, cf0, cs0, h0 = head_m(2 * s)
            m1, cf1, cs1, h1 = head_m(2 * s + 1)
            x_pair = xs_act[0, :, pl.ds(off, LANES)]
            x_bd = _block_diag_pair(x_pair, lane_lo)
            lhs_m = jnp.concatenate([m0, m1], axis=1)
            lhs_s = jnp.concatenate([b_t * row(ROW_W, h0), b_t * row(ROW_W, h1)],
                                    axis=1).astype(BF16)
            res = jnp.dot(jnp.concatenate([lhs_m, lhs_s], axis=0), x_bd,
                          preferred_element_type=F32)
            e_f = jnp.exp(jnp.where(lane_lo, cf0, cf1))
            e_b = jnp.exp(jnp.where(lane_lo, cs0, cs1))
            y = (res[0:CHUNK]
                 + yoff_f[:, s * LANES:(s + 1) * LANES] * e_f
                 + yoffb_ref[0, :, pl.ds(off, LANES)].astype(F32) * e_b
                 + dexp_ref[:, pl.ds(off, LANES)] * x_pair.astype(F32))
            drow = jnp.where(lane_lo, dec_ref[0, 0, pl.ds(h0, 1), :],
                             dec_ref[0, 0, pl.ds(h1, 1), :])
            s_ref[:, pl.ds(off, LANES)] = s_ref[:, pl.ds(off, LANES)] * drow + res[CHUNK:]
            gated = y * z_ref[0, :, pl.ds(off, LANES)].astype(F32)
            ybuf_ref[:, pl.ds(off, LANES)] = gated
            ssq_ref[...] += gated * gated
        return carry

    lax.fori_loop(0, N_GROUPS, group, 0)

    ms = jnp.sum(ssq_ref[...], axis=-1, keepdims=True) * (1.0 / D_INNER)
    inv = lax.rsqrt(ms + EPS)
    yn_ref[0] = (ybuf_ref[...] * inv * ng_ref[...]).astype(BF16)

    @pl.when(c == nc - 1)
    def _():
        sfin_ref[0] = s_ref[...]


def _pass2_call(xs_act, bc_act, main, yoffb, cols, rows, dec, s0, dexp, ng):
    b, l, _ = main.shape
    nc = l // CHUNK
    in_specs = [
        pl.BlockSpec((1, CHUNK, D_INNER), lambda i, c: (i, c, 0)),
        pl.BlockSpec((1, CHUNK, BC_DIM), lambda i, c: (i, c, 0)),
        pl.BlockSpec((1, CHUNK, D_INNER), lambda i, c: (i, c, 0)),
        pl.BlockSpec((1, CHUNK, D_INNER), lambda i, c: (i, c, 0)),
        pl.BlockSpec((1, CHUNK, LANES), lambda i, c: (i, c, 0)),
        pl.BlockSpec((1, 1, ROWS_TOTAL, CHUNK), lambda i, c: (i, c, 0, 0)),
        pl.BlockSpec((1, 1, LANES, LANES), lambda i, c: (i, c, 0, 0)),
        pl.BlockSpec((1, D_STATE, D_INNER), lambda i, c: (i, 0, 0)),
        pl.BlockSpec((1, D_INNER), lambda i, c: (0, 0)),
        pl.BlockSpec((1, D_INNER), lambda i, c: (0, 0)),
    ]
    out_specs = [
        pl.BlockSpec((1, CHUNK, D_INNER), lambda i, c: (i, c, 0)),
        pl.BlockSpec((1, D_STATE, D_INNER), lambda i, c: (i, 0, 0)),
    ]
    out_shape = [
        jax.ShapeDtypeStruct((b, l, D_INNER), BF16),
        jax.ShapeDtypeStruct((b, D_STATE, D_INNER), F32),
    ]
    return pl.pallas_call(
        functools.partial(_pass2_kernel, nc),
        grid=(b, nc),
        in_specs=in_specs,
        out_specs=out_specs,
        out_shape=out_shape,
        scratch_shapes=[
            pltpu.VMEM((D_STATE, D_INNER), F32),
            pltpu.VMEM((CHUNK, D_INNER), F32),
            pltpu.VMEM((CHUNK, LANES), F32),
        ],
        compiler_params=_cparams(("arbitrary", "arbitrary")),
        name="ssd_pass2",
    )(xs_act, bc_act, main, yoffb, cols, rows, dec, s0, dexp, ng)


def _oproj_kernel(y_ref, w_ref, x_ref, gate_ref, o_ref):
    acc = jnp.dot(y_ref[0], w_ref[...], preferred_element_type=F32)
    o_ref[0] = x_ref[0] + gate_ref[0] * acc


def _oproj_call(y, w, x, gate, tm=256):
    b, l, k = y.shape
    d = w.shape[1]
    return pl.pallas_call(
        _oproj_kernel,
        grid=(b, l // tm),
        in_specs=[
            pl.BlockSpec((1, tm, k), lambda i, t: (i, t, 0)),
            pl.BlockSpec((k, d), lambda i, t: (0, 0)),
            pl.BlockSpec((1, tm, d), lambda i, t: (i, t, 0)),
            pl.BlockSpec((1, 1, d), lambda i, t: (i, 0, 0)),
        ],
        out_specs=pl.BlockSpec((1, tm, d), lambda i, t: (i, t, 0)),
        out_shape=jax.ShapeDtypeStruct((b, l, d), F32),
        compiler_params=_cparams(("arbitrary", "arbitrary")),
        name="oproj",
    )(y, w, x, gate)


def _mixproj_kernel(u_ref, v_ref, g_ref, lng_ref, lnb_ref, ws_ref, bs_ref, w_ref, x_ref,
                    gate_ref, fg_ref, o_ref, s_ref):
    tm = u_ref.shape[1]
    for ch in range(tm // CHUNK):
        rows = slice(ch * CHUNK, (ch + 1) * CHUNK)
        v = v_ref[0, rows, :].astype(F32)
        mu = jnp.mean(v, axis=-1, keepdims=True)
        vc = v - mu
        var = jnp.mean(vc * vc, axis=-1, keepdims=True)
        vn = (vc * lax.rsqrt(var + EPS) * lng_ref[...] + lnb_ref[...]).astype(BF16)
        for g in range(SMLP_GROUPS):
            sl = slice(g * SMLP_GROUP_W, (g + 1) * SMLP_GROUP_W)
            vs = jnp.dot(ws_ref[g], vn[:, sl], preferred_element_type=F32) + bs_ref[:, sl]
            s_ref[rows, sl] = (u_ref[0, rows, sl].astype(F32) * vs
                               * g_ref[0, rows, sl].astype(F32)).astype(BF16)
    acc = jnp.dot(s_ref[...], w_ref[...], preferred_element_type=F32)
    r = x_ref[0] + gate_ref[0] * acc
    ms = jnp.mean(r * r, axis=-1, keepdims=True)
    o_ref[0] = r * lax.rsqrt(ms + EPS) * fg_ref[...]


def _mixproj_call(uvg, ln_g, ln_b, w_s, b_exp, w, x, gate, final_g, tm=256):
    b, l, _ = uvg.shape
    d = w.shape[1]
    const = pl.Buffered(1)
    return pl.pallas_call(
        _mixproj_kernel,
        grid=(b, l // tm),
        in_specs=[
            pl.BlockSpec((1, tm, D_INNER), lambda i, t: (i, t, 0)),
            pl.BlockSpec((1, tm, D_INNER), lambda i, t: (i, t, 1)),
            pl.BlockSpec((1, tm, D_INNER), lambda i, t: (i, t, 2)),
            pl.BlockSpec((1, D_INNER), lambda i, t: (0, 0)),
            pl.BlockSpec((1, D_INNER), lambda i, t: (0, 0)),
            pl.BlockSpec((SMLP_GROUPS, CHUNK, CHUNK), lambda i, t: (0, 0, 0), pipeline_mode=const),
            pl.BlockSpec((CHUNK, D_INNER), lambda i, t: (0, 0), pipeline_mode=const),
            pl.BlockSpec((D_INNER, d), lambda i, t: (0, 0), pipeline_mode=const),
            pl.BlockSpec((1, tm, d), lambda i, t: (i, t, 0)),
            pl.BlockSpec((1, 1, d), lambda i, t: (i, 0, 0)),
            pl.BlockSpec((1, d), lambda i, t: (0, 0)),
        ],
        out_specs=pl.BlockSpec((1, tm, d), lambda i, t: (i, t, 0)),
        out_shape=jax.ShapeDtypeStruct((b, l, d), F32),
        scratch_shapes=[pltpu.VMEM((tm, D_INNER), BF16)],
        compiler_params=_cparams(("arbitrary", "arbitrary")),
        name="smlp_mixproj",
    )(uvg, uvg, uvg, ln_g, ln_b, w_s, b_exp, w, x, gate, final_g)


def _ssd_sequence(main, dt_raw, bias, alog, s0_f, s0_b, conv, dexp, ng):
    wx, bx, wbc, bbc = conv
    cols, rows, dec = _dt_prep_call(dt_raw, bias, alog)
    xs_act, bc_act, yoffb, sfin_b = _pass1_call(main, rows, dec, s0_b, wx, bx, wbc, bbc)
    yn, sfin_f = _pass2_call(xs_act, bc_act, main, yoffb, cols, rows, dec, s0_f, dexp, ng)
    return yn, sfin_f, sfin_b


def kernel(x, c, ctx, c_ctx, mod_w, mod_b, ssd_w_in, ssd_conv_w, ssd_conv_b, ssd_dt_bias,
           ssd_a_log, ssd_d, ssd_norm_g, ssd_w_out, smlp_w_in, smlp_ln_g, smlp_ln_b,
           smlp_w_s, smlp_b_s, smlp_w_out, final_norm_g):
    b, l, d = x.shape
    lc = ctx.shape[1]

    cs = jnp.zeros((8, d), F32).at[:b].set(c).at[b].set(c_ctx)
    mods = _mod_call(cs, mod_w, mod_b)

    ss_lat = mods[0, :b, :2 * d].reshape(b, 2, d)
    ss_ctx = mods[0, b:b + 1, :2 * d].reshape(1, 2, d)
    gate0 = mods[0, :b, 2 * d:].reshape(b, 1, d)
    w_in = ssd_w_in[0]
    w_main = w_in[:, :MAIN_DIM].astype(BF16)
    w_dt = w_in[:, MAIN_DIM:].astype(BF16)
    ssd_acts = ((0, _silu), (D_INNER // 1024, _identity))
    main_l, dt_l = _nmm_call(x, ss_lat, w_main, ssd_acts, w_dt)
    main_c, dt_c = _nmm_call(ctx.reshape(1, b * lc, d), ss_ctx, w_main, ssd_acts, w_dt)
    main_c = main_c.reshape(b, lc, MAIN_DIM)
    dt_c = dt_c.reshape(b, lc, 2 * N_HEADS)

    bias = ssd_dt_bias[0].reshape(1, 2 * N_HEADS)
    alog = ssd_a_log[0].reshape(1, 2 * N_HEADS)
    conv_w = ssd_conv_w[0]
    conv_b = ssd_conv_b[0].reshape(1, -1)
    conv = (conv_w[:, :D_INNER], conv_b[:, :D_INNER], conv_w[:, D_INNER:], conv_b[:, D_INNER:])
    dexp = jnp.repeat(ssd_d[0], HEAD_DIM).reshape(1, D_INNER)
    ng = ssd_norm_g[0].reshape(1, D_INNER)

    zero_state = jnp.zeros((b, D_STATE, D_INNER), F32)
    _, sc_f, sc_b = _ssd_sequence(main_c, dt_c, bias, alog, zero_state, zero_state, conv, dexp, ng)
    yn, _, _ = _ssd_sequence(main_l, dt_l, bias, alog, sc_f, sc_b, conv, dexp, ng)
    x1 = _oproj_call(yn, ssd_w_out[0].astype(BF16), x, gate0)

    ss1 = mods[1, :b, :2 * d].reshape(b, 2, d)
    gate1 = mods[1, :b, 2 * d:].reshape(b, 1, d)
    smlp_acts = ((0, _gelu_tanh), (2 * D_INNER // 1024, _silu))
    uvg = _nmm_call(x1, ss1, smlp_w_in[0].astype(BF16), smlp_acts)[0]
    b_exp = jnp.repeat(smlp_b_s[0].T, SMLP_GROUP_W, axis=1)
    return _mixproj_call(uvg, smlp_ln_g[0].reshape(1, -1), smlp_ln_b[0].reshape(1, -1),
                         smlp_w_s[0].astype(BF16), b_exp, smlp_w_out[0].astype(BF16), x1, gate1,
                         final_norm_g.reshape(1, d))
```
